```python
import math
import jax, jax.numpy as jnp
from jax import lax
import numpy as np

D_MODEL = 1024
BATCH = 8
SEQ = 2048
DEPTH = 1

N_META = 16
POOL_WINDOWS = (2, 4, 8, 16)
POOL_GROUPS = 4
POOL_WIDTH = D_MODEL // 2
POOL_GC = POOL_WIDTH // POOL_GROUPS
N_HEADS = 8
HEAD_DIM = 64
N_KV = 2
Q_WIDTH = N_HEADS * HEAD_DIM
KV_WIDTH = N_KV * HEAD_DIM
IDX_HEADS = 4
IDX_DIM = 64
IDX_ROPE = 32
IDX_Q_WIDTH = IDX_HEADS * IDX_DIM
TOPK_CAP = 256
Q_BLOCK = 128
ROPE_THETA = 10000.0
N_KEYS = 128
N_EXPERTS = N_KEYS * N_KEYS
PEER_HEADS = 8
PEER_QDIM = 256
PEER_HALF = PEER_QDIM // 2
PEER_TOPK = 16
PEER_CHUNK = 128
LN_EPS = 1e-5
ALPHA = (2 * DEPTH) ** 0.25
BETA = (8 * DEPTH) ** -0.25

IN_SIZES = (POOL_WIDTH, Q_WIDTH, KV_WIDTH, KV_WIDTH, IDX_Q_WIDTH, IDX_DIM, IDX_HEADS, D_MODEL, D_MODEL)
IN_SPLITS = [sum(IN_SIZES[:i + 1]) for i in range(len(IN_SIZES) - 1)]
IN_WIDTH = sum(IN_SIZES)
V_OFFSET = IN_SPLITS[2]

kernel_name = "hybrid_pool_dsa_peer_deepnorm"


def layer_norm(x, g, b):
    xf = x.astype(jnp.float32)
    mu = jnp.mean(xf, axis=-1, keepdims=True)
    var = jnp.mean(jnp.square(xf - mu), axis=-1, keepdims=True)
    return ((xf - mu) * lax.rsqrt(var + LN_EPS) * g.astype(jnp.float32) + b.astype(jnp.float32)).astype(x.dtype)


def rope(x, pos, rot_dim):
    half = rot_dim // 2
    inv = ROPE_THETA ** (-jnp.arange(half, dtype=jnp.float32) / half)
    ang = pos.astype(jnp.float32)[:, None] * inv[None, :]
    cos = jnp.cos(ang)[None, :, None, :]
    sin = jnp.sin(ang)[None, :, None, :]
    xr = x[..., :rot_dim].astype(jnp.float32)
    x1, x2 = xr[..., :half], xr[..., half:]
    rot = jnp.concatenate([x1 * cos - x2 * sin, x2 * cos + x1 * sin], axis=-1).astype(x.dtype)
    return jnp.concatenate([rot, x[..., rot_dim:]], axis=-1)


def pool_mixer(xp, w_group, scale):
    B, T, _ = xp.shape
    xg = xp.reshape(B, T, POOL_GROUPS, POOL_GC).astype(jnp.float32)
    cs = jnp.cumsum(xg, axis=1)
    t1 = jnp.arange(1, T + 1, dtype=jnp.float32)
    pooled = []
    for g, w in enumerate(POOL_WINDOWS):
        c = cs[:, :, g]
        lag = jnp.pad(c, ((0, 0), (w, 0), (0, 0)))[:, :T]
        cnt = jnp.minimum(t1, float(w))[None, :, None]
        pooled.append((c - lag) / cnt)
    mixed = (jnp.stack(pooled, axis=2) - xg).astype(xp.dtype)
    y = jnp.einsum('btgc,gcd->btgd', mixed, w_group).reshape(B, T, POOL_WIDTH)
    return y * scale


def dsa_attention(q, k, v, qi, ki, wi, topk):
    B, T = q.shape[0], q.shape[1]
    nb = T // Q_BLOCK
    pos = jnp.arange(T)
    q = rope(q, pos, HEAD_DIM)
    k = rope(k, pos, HEAD_DIM)
    qi = rope(qi, pos, IDX_ROPE)
    ki = rope(ki[:, :, None, :], pos, IDX_ROPE)[:, :, 0, :]
    wi = wi * (IDX_HEADS ** -0.5 * IDX_DIM ** -0.5)

    def to_blocks(a):
        return a.reshape(B, nb, Q_BLOCK, *a.shape[2:]).swapaxes(0, 1)

    def block(args):
        qb, qib, wb, qpos = args
        iscore = jax.nn.relu(jnp.einsum('bqhd,bsd->bqhs', qib, ki).astype(jnp.float32))
        isum = jnp.einsum('bqh,bqhs->bqs', wb.astype(jnp.float32), iscore)
        causal = pos[None, None, :] <= qpos[None, :, None]
        isum = jnp.where(causal, isum, -jnp.inf)
        _, idx = lax.top_k(isum, topk)
        valid = idx <= qpos[None, :, None]
        kg = jax.vmap(lambda kb, ib: kb[ib])(k, idx)
        vg = jax.vmap(lambda vb, ib: vb[ib])(v, idx)
        qg = qb.reshape(B, Q_BLOCK, N_KV, N_HEADS // N_KV, HEAD_DIM)
        s = jnp.einsum('bqgrd,bqkgd->bqgrk', qg, kg).astype(jnp.float32) * (HEAD_DIM ** -0.5)
        s = jnp.where(valid[:, :, None, None, :], s, -jnp.inf)
        p = jax.nn.softmax(s, axis=-1).astype(v.dtype)
        o = jnp.einsum('bqgrk,bqkgd->bqgrd', p, vg)
        return o.reshape(B, Q_BLOCK, Q_WIDTH)

    out = lax.map(block, (to_blocks(q), to_blocks(qi), to_blocks(wi), pos.reshape(nb, Q_BLOCK)))
    return out.swapaxes(0, 1).reshape(B, T, Q_WIDTH)


def mixer_block(h, w_in, b_gate, pool_w, pool_scale, w_pool_up, w_attn_o, w_out, topk):
    B, T, _ = h.shape
    z = h @ w_in
    xp, q, k, v, qi, ki, wi, gp, ga = jnp.split(z, IN_SPLITS, axis=-1)
    pool_out = pool_mixer(xp, pool_w, pool_scale) @ w_pool_up
    attn = dsa_attention(q.reshape(B, T, N_HEADS, HEAD_DIM), k.reshape(B, T, N_KV, HEAD_DIM),
                         v.reshape(B, T, N_KV, HEAD_DIM), qi.reshape(B, T, IDX_HEADS, IDX_DIM),
                         ki, wi, topk) @ w_attn_o
    merged = (jax.nn.sigmoid(gp + b_gate[:D_MODEL]) * pool_out
              + jax.nn.sigmoid(ga + b_gate[D_MODEL:]) * attn)
    return merged @ w_out


def peer(h, w_q, sub_keys, u_tab, v_tab):
    B, T, D = h.shape
    n = B * T
    xf = h.reshape(n, D)
    q = (xf @ w_q).reshape(n, PEER_HEADS, 2, PEER_HALF)
    s1 = jnp.einsum('nhd,hkd->nhk', q[:, :, 0], sub_keys[:, 0])
    s2 = jnp.einsum('nhd,hkd->nhk', q[:, :, 1], sub_keys[:, 1])
    v1, i1 = lax.top_k(s1, PEER_TOPK)
    v2, i2 = lax.top_k(s2, PEER_TOPK)
    cand = (v1[..., :, None] + v2[..., None, :]).reshape(n, PEER_HEADS, PEER_TOPK * PEER_TOPK)
    cidx = (i1[..., :, None] * N_KEYS + i2[..., None, :]).reshape(n, PEER_HEADS, PEER_TOPK * PEER_TOPK)
    top, sel = lax.top_k(cand, PEER_TOPK)
    eidx = jnp.take_along_axis(cidx, sel, axis=-1)
    gate = jax.nn.softmax(top.astype(jnp.float32), axis=-1).astype(h.dtype)
    nc = n // PEER_CHUNK

    def chunk(args):
        xc, ec, gc = args
        act = jax.nn.gelu(jnp.einsum('cd,ced->ce', xc, u_tab[ec]), approximate=False)
        return jnp.einsum('ce,ced->cd', gc * act, v_tab[ec])

    y = lax.map(chunk, (xf.reshape(nc, PEER_CHUNK, D),
                        eidx.reshape(nc, PEER_CHUNK, PEER_HEADS * PEER_TOPK),
                        gate.reshape(nc, PEER_CHUNK, PEER_HEADS * PEER_TOPK)))
    return y.reshape(B, T, D)


def setup_inputs(seed: int = 0) -> dict:
    key = jax.random.key(seed)
    ks = jax.random.split(key, 20)
    f32 = jnp.float32
    D = D_MODEL
    nrm = lambda k, shape, s: jax.random.normal(k, shape, f32) * s
    col_scale = jnp.ones((IN_WIDTH,), f32).at[V_OFFSET:V_OFFSET + KV_WIDTH].set(BETA)
    return {
        "x": nrm(ks[0], (BATCH, SEQ, D), 1.0),
        "meta_tokens": nrm(ks[1], (N_META, D), 1.0),
        "ln0_g": 1.0 + nrm(ks[2], (D,), 0.02),
        "ln0_b": nrm(ks[3], (D,), 0.02),
        "w_in": nrm(ks[4], (DEPTH, D, IN_WIDTH), D ** -0.5) * col_scale,
        "b_gate": nrm(ks[5], (DEPTH, 2 * D), 0.02),
        "pool_w": nrm(ks[6], (DEPTH, POOL_GROUPS, POOL_GC, POOL_GC), POOL_GC ** -0.5),
        "pool_scale": 1.0 + nrm(ks[7], (DEPTH, POOL_WIDTH), 0.02),
        "w_pool_up": nrm(ks[8], (DEPTH, POOL_WIDTH, D), POOL_WIDTH ** -0.5 * BETA),
        "w_attn_o": nrm(ks[9], (DEPTH, Q_WIDTH, D), Q_WIDTH ** -0.5 * BETA),
        "w_out": nrm(ks[10], (DEPTH, D, D), D ** -0.5 * BETA),
        "ln1_g": 1.0 + nrm(ks[11], (DEPTH, D), 0.02),
        "ln1_b": nrm(ks[12], (DEPTH, D), 0.02),
        "peer_wq": nrm(ks[13], (DEPTH, D, PEER_HEADS * PEER_QDIM), D ** -0.5),
        "peer_sub_keys": nrm(ks[14], (DEPTH, PEER_HEADS, 2, N_KEYS, PEER_HALF), PEER_HALF ** -0.5),
        "peer_u": nrm(ks[15], (DEPTH, N_EXPERTS, D), D ** -0.5),
        "peer_v": nrm(ks[16], (DEPTH, N_EXPERTS, D), BETA),
        "ln2_g": 1.0 + nrm(ks[17], (DEPTH, D), 0.02),
        "ln2_b": nrm(ks[18], (DEPTH, D), 0.02),
    }


def reference(x, meta_tokens, ln0_g, ln0_b, w_in, b_gate, pool_w, pool_scale, w_pool_up,
              w_attn_o, w_out, ln1_g, ln1_b, peer_wq, peer_sub_keys, peer_u, peer_v, ln2_g, ln2_b):
    B, S, D = x.shape
    T = N_META + S
    T_pad = -(-T // Q_BLOCK) * Q_BLOCK
    meta = jnp.broadcast_to(meta_tokens.astype(x.dtype)[None], (B, N_META, D))
    h = jnp.concatenate([meta, x, jnp.zeros((B, T_pad - T, D), x.dtype)], axis=1)
    h = layer_norm(h, ln0_g, ln0_b)
    topk = min(TOPK_CAP, S // 4)
    for l in range(DEPTH):
        mix = mixer_block(h, w_in[l], b_gate[l], pool_w[l], pool_scale[l], w_pool_up[l],
                          w_attn_o[l], w_out[l], topk)
        h = layer_norm(ALPHA * h + mix, ln1_g[l], ln1_b[l])
        ffn = peer(h, peer_wq[l], peer_sub_keys[l], peer_u[l], peer_v[l])
        h = layer_norm(ALPHA * h + ffn, ln2_g[l], ln2_b[l])
    return h[:, N_META:N_META + S]
```

```python
import functools
import math

import jax
import jax.numpy as jnp
from jax import lax
from jax.experimental import pallas as pl
from jax.experimental.pallas import tpu as pltpu

F32 = jnp.float32
BF16 = jnp.bfloat16

N_META = 16
POOL_WINDOWS = (2, 4, 8, 16)
POOL_GC = 128
POOL_WIDTH = POOL_GC * len(POOL_WINDOWS)
N_HEADS = 8
HEAD_DIM = 64
N_KV = 2
HEADS_PER_KV = N_HEADS // N_KV
IDX_HEADS = 4
IDX_DIM = 64
IDX_ROPE = 32
TOPK_CAP = 256
Q_BLOCK = 128
ROPE_THETA = 10000.0
N_KEYS = 128
PEER_HEADS = 8
PEER_HALF = 128
PEER_TOPK = 16
LN_EPS = 1e-5

LANES = 128
SUBLANES = 8
VMEM_LIMIT_BYTES = 56 * 1024 * 1024

NEG_INF = float("-inf")
POS_INF = float("inf")
BISECT_STEPS = 16

PEER_PAIRS = tuple((a, b) for a in range(PEER_TOPK) for b in range(PEER_TOPK)
                   if (a + 1) * (b + 1) <= PEER_TOPK)


def _pick_tile(n, candidates):
    for c in candidates:
        if n % c == 0:
            return c
    raise ValueError(f"no tile in {candidates} divides {n}")


def _cparams(sem):
    return pltpu.CompilerParams(dimension_semantics=sem, vmem_limit_bytes=VMEM_LIMIT_BYTES)


def _dot(a, b):
    return jnp.dot(a, b, preferred_element_type=F32)


def _dot_nt(a, b):
    return lax.dot_general(a, b, (((1,), (1,)), ((), ())), preferred_element_type=F32)


def _layer_norm_rows(x, g, b):
    mu = jnp.mean(x, axis=-1, keepdims=True)
    xc = x - mu
    var = jnp.mean(xc * xc, axis=-1, keepdims=True)
    return xc * lax.rsqrt(var + LN_EPS) * g + b


def _ln_kernel(x_ref, g_ref, b_ref, o_ref):
    o_ref[...] = _layer_norm_rows(x_ref[...], g_ref[...], b_ref[...])


def _layer_norm(x2, g, b):
    n, d = x2.shape
    tm = _pick_tile(n, (512, 256, 128))
    return pl.pallas_call(
        _ln_kernel,
        name="ln0",
        grid=(n // tm,),
        in_specs=[pl.BlockSpec((tm, d), lambda i: (i, 0)),
                  pl.BlockSpec((1, d), lambda i: (0, 0)),
                  pl.BlockSpec((1, d), lambda i: (0, 0))],
        out_specs=pl.BlockSpec((tm, d), lambda i: (i, 0)),
        out_shape=jax.ShapeDtypeStruct((n, d), F32),
        compiler_params=_cparams(("parallel",)),
    )(x2, g.reshape(1, d), b.reshape(1, d))


C_XP = 0
C_Q = C_XP + POOL_WIDTH
C_QS = C_Q + N_HEADS * LANES
C_K = C_QS + N_HEADS * LANES
C_KS = C_K + N_KV * LANES
C_V = C_KS + N_KV * LANES
C_QI = C_V + N_KV * HEAD_DIM
C_QIS = C_QI + IDX_HEADS * LANES
C_KI = C_QIS + IDX_HEADS * LANES
C_KIS = C_KI + LANES
C_WI = C_KIS + LANES
C_END = C_WI + LANES
WI_SCALE = IDX_HEADS ** -0.5 * IDX_DIM ** -0.5


def _inproj_kernel(h_ref, w_ref, cq_ref, sq_ref, ck_ref, sk_ref, ci_ref, si_ref,
                   xp_ref, q_ref, k_ref, v_ref, qi_ref, ki_ref, wi_ref):
    hb = h_ref[...].astype(BF16)

    def seg(c0, width):
        return _dot(hb, w_ref[:, c0:c0 + width])

    xp_ref[...] = seg(C_XP, POOL_WIDTH)
    v_ref[...] = seg(C_V, N_KV * HEAD_DIM).astype(BF16)
    wi_ref[...] = seg(C_WI, LANES) * WI_SCALE
    cq, sq = cq_ref[...], sq_ref[...]
    ck, sk = ck_ref[...], sk_ref[...]
    ci, si = ci_ref[...], si_ref[...]
    for hh in range(N_HEADS):
        a = seg(C_Q + hh * LANES, LANES)
        s = seg(C_QS + hh * LANES, LANES)
        q_ref[:, hh * LANES:(hh + 1) * LANES] = (a * cq + s * sq).astype(BF16)
    for g in range(N_KV):
        a = seg(C_K + g * LANES, LANES)
        s = seg(C_KS + g * LANES, LANES)
        k_ref[:, g * LANES:(g + 1) * LANES] = (a * ck + s * sk).astype(BF16)
    for hh in range(IDX_HEADS):
        a = seg(C_QI + hh * LANES, LANES)
        s = seg(C_QIS + hh * LANES, LANES)
        qi_ref[:, hh * LANES:(hh + 1) * LANES] = (a * ci + s * si).astype(BF16)
    ki_ref[...] = (seg(C_KI, LANES) * ci + seg(C_KIS, LANES) * si).astype(BF16)


def _pad_heads(w, n_heads, hd):
    d = w.shape[0]
    w3 = w.reshape(d, n_heads, hd)
    return jnp.pad(w3, ((0, 0), (0, 0), (0, LANES - hd))).reshape(d, n_heads * LANES)


def _swap_halves(w, n_heads, hd, rot):
    d = w.shape[0]
    half = rot // 2
    w3 = w.reshape(d, n_heads, hd)
    sw = jnp.concatenate([w3[..., half:rot], w3[..., :half], jnp.zeros_like(w3[..., rot:])], axis=-1)
    return sw.reshape(d, n_heads * hd)


def _rope_tables(t_pad, rot, scale):
    half = rot // 2
    inv = ROPE_THETA ** (-jnp.arange(half, dtype=F32) / half)
    ang = jnp.arange(t_pad, dtype=F32)[:, None] * inv[None, :]
    cos, sin = jnp.cos(ang), jnp.sin(ang)
    ones = jnp.ones((t_pad, LANES - rot), F32)
    c = jnp.concatenate([cos, cos, ones], axis=1) * scale
    s = jnp.concatenate([-sin, sin, jnp.zeros((t_pad, LANES - rot), F32)], axis=1) * scale
    return c, s


def _build_inproj_weight(w_in):
    o = 0
    parts = {}
    for name, width in (("xp", POOL_WIDTH), ("q", N_HEADS * HEAD_DIM), ("k", N_KV * HEAD_DIM),
                        ("v", N_KV * HEAD_DIM), ("qi", IDX_HEADS * IDX_DIM), ("ki", IDX_DIM),
                        ("wi", IDX_HEADS)):
        parts[name] = w_in[:, o:o + width]
        o += width
    d = w_in.shape[0]
    cols = [
        parts["xp"],
        _pad_heads(parts["q"], N_HEADS, HEAD_DIM),
        _pad_heads(_swap_halves(parts["q"], N_HEADS, HEAD_DIM, HEAD_DIM), N_HEADS, HEAD_DIM),
        _pad_heads(parts["k"], N_KV, HEAD_DIM),
        _pad_heads(_swap_halves(parts["k"], N_KV, HEAD_DIM, HEAD_DIM), N_KV, HEAD_DIM),
        parts["v"],
        _pad_heads(parts["qi"], IDX_HEADS, IDX_DIM),
        _pad_heads(_swap_halves(parts["qi"], IDX_HEADS, IDX_DIM, IDX_ROPE), IDX_HEADS, IDX_DIM),
        _pad_heads(parts["ki"], 1, IDX_DIM),
        _pad_heads(_swap_halves(parts["ki"], 1, IDX_DIM, IDX_ROPE), 1, IDX_DIM),
        jnp.pad(parts["wi"], ((0, 0), (0, LANES - IDX_HEADS))),
    ]
    w = jnp.concatenate(cols, axis=1).astype(BF16)
    assert w.shape == (d, C_END)
    return w, o


def _inproj(hn, w1, tables):
    b, t_pad, d = hn.shape
    tm = _pick_tile(t_pad, (272, 136, 128))
    row = lambda width: pl.BlockSpec((None, tm, width), lambda bi, i: (bi, i, 0))
    tab = pl.BlockSpec((tm, LANES), lambda bi, i: (i, 0))
    outs = [(POOL_WIDTH, F32), (N_HEADS * LANES, BF16), (N_KV * LANES, BF16), (N_KV * HEAD_DIM, BF16),
            (IDX_HEADS * LANES, BF16), (LANES, BF16), (LANES, F32)]
    return pl.pallas_call(
        _inproj_kernel,
        name="inproj",
        grid=(b, t_pad // tm),
        in_specs=[row(d), pl.BlockSpec((d, C_END), lambda bi, i: (0, 0))] + [tab] * 6,
        out_specs=[row(wd) for wd, _ in outs],
        out_shape=[jax.ShapeDtypeStruct((b, t_pad, wd), dt) for wd, dt in outs],
        compiler_params=_cparams(("parallel", "parallel")),
    )(hn, w1, *tables)


HALO = 16


def _pool_kernel(xp_ref, pw_ref, ps_ref, wup_ref, o_ref, ext_ref, *, tm):
    i = pl.program_id(1)

    @pl.when(i == 0)
    def _():
        ext_ref[0:HALO, :] = jnp.zeros((HALO, POOL_WIDTH), F32)

    @pl.when(i > 0)
    def _():
        ext_ref[0:HALO, :] = ext_ref[tm:tm + HALO, :]

    ext_ref[HALO:HALO + tm, :] = xp_ref[...]
    t1 = (i * tm + 1 + lax.broadcasted_iota(jnp.int32, (tm, 1), 0)).astype(F32)
    ys = []
    for g, w in enumerate(POOL_WINDOWS):
        lanes = slice(g * POOL_GC, (g + 1) * POOL_GC)
        x = ext_ref[HALO:HALO + tm, lanes]
        acc = x
        for j in range(1, w):
            acc = acc + ext_ref[HALO - j:HALO - j + tm, lanes]
        cnt = jnp.minimum(t1, float(w))
        mixed = acc / cnt - x
        ys.append(_dot(mixed.astype(BF16), pw_ref[g]))
    y = jnp.concatenate(ys, axis=1) * ps_ref[...]
    o_ref[...] = _dot(y.astype(BF16), wup_ref[...])


def _pool(xp, pool_w, pool_scale, w_pool_up):
    b, t_pad, _ = xp.shape
    d = w_pool_up.shape[1]
    tm = _pick_tile(t_pad, (272, 136, 128))
    return pl.pallas_call(
        functools.partial(_pool_kernel, tm=tm),
        name="pool",
        grid=(b, t_pad // tm),
        in_specs=[pl.BlockSpec((None, tm, POOL_WIDTH), lambda bi, i: (bi, i, 0)),
                  pl.BlockSpec((len(POOL_WINDOWS), POOL_GC, POOL_GC), lambda bi, i: (0, 0, 0)),
                  pl.BlockSpec((1, POOL_WIDTH), lambda bi, i: (0, 0)),
                  pl.BlockSpec((POOL_WIDTH, d), lambda bi, i: (0, 0))],
        out_specs=pl.BlockSpec((None, tm, d), lambda bi, i: (bi, i, 0)),
        out_shape=jax.ShapeDtypeStruct((b, t_pad, d), F32),
        scratch_shapes=[pltpu.VMEM((tm + HALO, POOL_WIDTH), F32)],
        compiler_params=_cparams(("parallel", "arbitrary")),
    )(xp, pool_w.astype(BF16), pool_scale.reshape(1, POOL_WIDTH), w_pool_up.astype(BF16))


def _attn_kernel(qi_ref, wi_ref, ki_ref, q_ref, k_ref, v_ref, o_ref,
                 vt_ref, isum_ref, bias_ref, sc_ref, ot_ref, *, nb, topk):
    qb = pl.program_id(1)
    nkb = qb + 1
    kf = float(topk)

    @pl.when(qb == 0)
    def _():
        for c in range(nb):
            blk = v_ref[c * Q_BLOCK:(c + 1) * Q_BLOCK, :].astype(F32)
            vt_ref[c] = blk.T.astype(BF16)

    rows = lax.broadcasted_iota(jnp.int32, (Q_BLOCK, Q_BLOCK), 0)
    cols = lax.broadcasted_iota(jnp.int32, (Q_BLOCK, Q_BLOCK), 1)
    tri_causal = rows <= cols
    tri_prefix = (cols <= rows).astype(BF16)

    wt = wi_ref[...].T
    qis = [qi_ref[:, hh * LANES:(hh + 1) * LANES] for hh in range(IDX_HEADS)]
    wrows = [wt[hh:hh + 1, :] for hh in range(IDX_HEADS)]

    def idx_body(kb, carry):
        xmax, xmin = carry
        start = pl.multiple_of(kb * Q_BLOCK, Q_BLOCK)
        kic = ki_ref[pl.ds(start, Q_BLOCK), :]
        acc = jnp.zeros((Q_BLOCK, Q_BLOCK), F32)
        for hh in range(IDX_HEADS):
            acc = acc + wrows[hh] * jnp.maximum(_dot_nt(kic, qis[hh]), 0.0)
        causal = jnp.logical_or(kb < qb, tri_causal)
        isum_ref[kb] = jnp.where(causal, acc, NEG_INF)
        xmax = jnp.maximum(xmax, jnp.max(jnp.where(causal, acc, NEG_INF), axis=0, keepdims=True))
        xmin = jnp.minimum(xmin, jnp.min(jnp.where(causal, acc, POS_INF), axis=0, keepdims=True))
        return xmax, xmin

    xmax, xmin = lax.fori_loop(
        0, nkb, idx_body,
        (jnp.full((1, Q_BLOCK), NEG_INF, F32), jnp.full((1, Q_BLOCK), POS_INF, F32)))

    def count_ge(t):
        def body(kb, c):
            return c + jnp.sum(jnp.where(isum_ref[kb] >= t, 1.0, 0.0), axis=0, keepdims=True)
        return lax.fori_loop(0, nkb, body, jnp.zeros((1, Q_BLOCK), F32))

    def count_and_next(t):
        def body(kb, c):
            cnt, nxt = c
            x = isum_ref[kb]
            cnt = cnt + jnp.sum(jnp.where(x >= t, 1.0, 0.0), axis=0, keepdims=True)
            nxt = jnp.maximum(nxt, jnp.max(jnp.where(x < t, x, NEG_INF), axis=0, keepdims=True))
            return cnt, nxt
        return lax.fori_loop(0, nkb, body, (jnp.zeros((1, Q_BLOCK), F32),
                                            jnp.full((1, Q_BLOCK), NEG_INF, F32)))

    def search(_):
        c_hi = count_ge(xmax)

        def bis(_, c):
            lo, hi = c
            mid = lo + 0.5 * (hi - lo)
            ge = count_ge(mid) >= kf
            return jnp.where(ge, mid, lo), jnp.where(ge, hi, mid)

        lo, hi = lax.fori_loop(0, BISECT_STEPS, bis, (xmin, xmax))
        _, below_hi = count_and_next(hi)
        t0 = jnp.where(c_hi >= kf, xmax, below_hi)

        def wcond(c):
            return c[1] > 0.0

        def wbody(c):
            t, _ = c
            cnt, nxt = count_and_next(t)
            need = cnt < kf
            return jnp.where(need, nxt, t), jnp.max(jnp.where(need, 1.0, 0.0))

        thr, _ = lax.while_loop(wcond, wbody, (t0, jnp.float32(1.0)))
        return thr

    thr = lax.cond(nkb * Q_BLOCK > topk, search,
                   lambda _: jnp.full((1, Q_BLOCK), NEG_INF, F32), 0)

    def count_gt(kb, c):
        return c + jnp.sum(jnp.where(isum_ref[kb] > thr, 1.0, 0.0), axis=0, keepdims=True)

    room = kf - lax.fori_loop(0, nkb, count_gt, jnp.zeros((1, Q_BLOCK), F32))
    tie_ok = thr > NEG_INF

    def bias_body(kb, carry):
        x = isum_ref[kb]
        eq = x == thr
        pref = _dot(tri_prefix, jnp.where(eq, 1.0, 0.0).astype(BF16)) + carry
        sel = jnp.logical_or(x > thr, jnp.logical_and(jnp.logical_and(eq, tie_ok), pref <= room))
        bias_ref[kb] = jnp.where(sel, 0.0, NEG_INF)
        return pref[Q_BLOCK - 1:Q_BLOCK, :]

    lax.fori_loop(0, nkb, bias_body, jnp.zeros((1, Q_BLOCK), F32))

    for hh in range(N_HEADS):
        g = hh // HEADS_PER_KV
        qh = q_ref[:, hh * LANES:(hh + 1) * LANES]

        def s_body(kb, m):
            start = pl.multiple_of(kb * Q_BLOCK, Q_BLOCK)
            kc = k_ref[pl.ds(start, Q_BLOCK), g * LANES:(g + 1) * LANES]
            s = _dot_nt(kc, qh) + bias_ref[kb]
            sc_ref[kb] = s
            return jnp.maximum(m, jnp.max(s, axis=0, keepdims=True))

        m = lax.fori_loop(0, nkb, s_body, jnp.full((1, Q_BLOCK), NEG_INF, F32))

        def p_body(kb, c):
            l, acc = c
            p = jnp.exp(sc_ref[kb] - m)
            l = l + jnp.sum(p, axis=0, keepdims=True)
            vt = vt_ref[kb][g * HEAD_DIM:(g + 1) * HEAD_DIM, :]
            return l, acc + _dot(vt, p.astype(BF16))

        l, acc = lax.fori_loop(0, nkb, p_body, (jnp.zeros((1, Q_BLOCK), F32),
                                                jnp.zeros((HEAD_DIM, Q_BLOCK), F32)))
        ot_ref[hh * HEAD_DIM:(hh + 1) * HEAD_DIM, :] = acc / l

    o_ref[...] = ot_ref[...].T.astype(BF16)


def _attention(qi, wi, ki, q, k, v, topk):
    b, t_pad, _ = q.shape
    nb = t_pad // Q_BLOCK
    blk = lambda width: pl.BlockSpec((None, Q_BLOCK, width), lambda bi, i: (bi, i, 0))
    full = lambda width: pl.BlockSpec((None, t_pad, width), lambda bi, i: (bi, 0, 0))
    qw = N_HEADS * HEAD_DIM
    return pl.pallas_call(
        functools.partial(_attn_kernel, nb=nb, topk=topk),
        name="dsa_attn",
        grid=(b, nb),
        in_specs=[blk(IDX_HEADS * LANES), blk(LANES), full(LANES),
                  blk(N_HEADS * LANES), full(N_KV * LANES), full(N_KV * HEAD_DIM)],
        out_specs=blk(qw),
        out_shape=jax.ShapeDtypeStruct((b, t_pad, qw), BF16),
        scratch_shapes=[pltpu.VMEM((nb, N_KV * HEAD_DIM, Q_BLOCK), BF16),
                        pltpu.VMEM((nb, Q_BLOCK, Q_BLOCK), F32),
                        pltpu.VMEM((nb, Q_BLOCK, Q_BLOCK), F32),
                        pltpu.VMEM((nb, Q_BLOCK, Q_BLOCK), F32),
                        pltpu.VMEM((qw, Q_BLOCK), F32)],
        compiler_params=_cparams(("parallel", "arbitrary")),
    )(qi, wi, ki, q, k, v)


def _merge_kernel(hn_ref, attn_ref, pool_ref, wg_ref, bg_ref, wo_ref, wout_ref, g1_ref, b1_ref,
                  wq_ref, sk_ref, h1_ref, st_ref, *, alpha, d):
    hn = hn_ref[...]
    gates = _dot(hn.astype(BF16), wg_ref[...]) + bg_ref[...]
    attn = _dot(attn_ref[...], wo_ref[...])
    merged = jax.nn.sigmoid(gates[:, :d]) * pool_ref[...] + jax.nn.sigmoid(gates[:, d:]) * attn
    mix = _dot(merged.astype(BF16), wout_ref[...])
    h1 = _layer_norm_rows(alpha * hn + mix, g1_ref[...], b1_ref[...])
    h1_ref[...] = h1
    qp = _dot(h1.astype(BF16), wq_ref[...]).astype(BF16)
    for c in range(2 * PEER_HEADS):
        st_ref[c] = _dot_nt(sk_ref[c], qp[:, c * PEER_HALF:(c + 1) * PEER_HALF])


def _merge(hn2, attn2, pool2, w_g, b_gate, w_attn_o, w_out, ln1_g, ln1_b, peer_wq, sub_keys, alpha):
    n, d = hn2.shape
    tm = _pick_tile(n, (256, 128))
    row = lambda width: pl.BlockSpec((tm, width), lambda i: (i, 0))
    const = lambda shape: pl.BlockSpec(shape, lambda i: (0,) * len(shape))
    nsk = 2 * PEER_HEADS
    qw = attn2.shape[1]
    return pl.pallas_call(
        functools.partial(_merge_kernel, alpha=alpha, d=d),
        name="merge",
        grid=(n // tm,),
        in_specs=[row(d), row(qw), row(d),
                  const((d, 2 * d)), const((1, 2 * d)), const((qw, d)), const((d, d)),
                  const((1, d)), const((1, d)), const((d, nsk * PEER_HALF)),
                  const((nsk, N_KEYS, PEER_HALF))],
        out_specs=[row(d), pl.BlockSpec((nsk, N_KEYS, tm), lambda i: (0, 0, i))],
        out_shape=[jax.ShapeDtypeStruct((n, d), F32),
                   jax.ShapeDtypeStruct((nsk, N_KEYS, n), F32)],
        compiler_params=_cparams(("parallel",)),
    )(hn2, attn2, pool2, w_g.astype(BF16), b_gate.reshape(1, 2 * d), w_attn_o.astype(BF16),
      w_out.astype(BF16), ln1_g.reshape(1, d), ln1_b.reshape(1, d), peer_wq.astype(BF16),
      sub_keys.reshape(nsk, N_KEYS, PEER_HALF).astype(BF16))


def _top_values(x, k):
    out = []
    for _ in range(k):
        m = jnp.max(x, axis=0, keepdims=True)
        out.append(m)
        x = jnp.where(x == m, NEG_INF, x)
    return out


def _route_kernel(st_ref, rt_ref):
    kf = float(PEER_TOPK)
    for hh in range(PEER_HEADS):
        v1 = _top_values(st_ref[2 * hh], PEER_TOPK)
        v2 = _top_values(st_ref[2 * hh + 1], PEER_TOPK)
        pad_rows = -len(PEER_PAIRS) % SUBLANES
        cand = jnp.concatenate([v1[a] + v2[b] for a, b in PEER_PAIRS]
                               + [jnp.full_like(v1[0], NEG_INF)] * pad_rows, axis=0)
        top = v1[0] + v2[0]
        x = cand
        theta = jnp.full_like(top, NEG_INF)
        left = jnp.full_like(top, kf)
        for _ in range(PEER_TOPK):
            m = jnp.max(x, axis=0, keepdims=True)
            hit = x == m
            theta = jnp.where(left > 0.0, m, theta)
            left = left - jnp.sum(jnp.where(hit, 1.0, 0.0), axis=0, keepdims=True)
            x = jnp.where(hit, NEG_INF, x)
        z = jnp.sum(jnp.where(cand >= theta, jnp.exp(cand - top), 0.0), axis=0, keepdims=True)
        rt_ref[hh:hh + 1, :] = theta
        rt_ref[PEER_HEADS + hh:PEER_HEADS + hh + 1, :] = v1[0]
        rt_ref[2 * PEER_HEADS + hh:2 * PEER_HEADS + hh + 1, :] = v2[0]
        rt_ref[3 * PEER_HEADS + hh:3 * PEER_HEADS + hh + 1, :] = 1.0 / z


def _route(st):
    nsk, nk, n = st.shape
    tb = _pick_tile(n, (256, 128))
    return pl.pallas_call(
        _route_kernel,
        name="peer_route",
        grid=(n // tb,),
        in_specs=[pl.BlockSpec((nsk, nk, tb), lambda i: (0, 0, i))],
        out_specs=pl.BlockSpec((4 * PEER_HEADS, tb), lambda i: (0, i)),
        out_shape=jax.ShapeDtypeStruct((4 * PEER_HEADS, n), F32),
        compiler_params=_cparams(("parallel",)),
    )(st)


def _peer_kernel(h_ref, st_ref, rt_ref, u_ref, v_ref, g2_ref, b2_ref, o_ref,
                 xb_ref, e2_ref, *, alpha):
    j = pl.program_id(1)
    nj = pl.num_programs(1)

    @pl.when(j == 0)
    def _():
        xb_ref[...] = h_ref[...].astype(BF16)
        o_ref[...] = jnp.zeros_like(o_ref)
        for hh in range(PEER_HEADS):
            m2 = rt_ref[2 * PEER_HEADS + hh:2 * PEER_HEADS + hh + 1, :]
            zinv = rt_ref[3 * PEER_HEADS + hh:3 * PEER_HEADS + hh + 1, :]
            e2_ref[hh] = jnp.exp(st_ref[2 * hh + 1] - m2) * zinv

    at = _dot_nt(u_ref[...].astype(BF16), xb_ref[...])
    gt = jnp.zeros_like(at)
    for hh in range(PEER_HEADS):
        s1 = st_ref[2 * hh, pl.ds(j, 1), :]
        theta = rt_ref[hh:hh + 1, :]
        m1 = rt_ref[PEER_HEADS + hh:PEER_HEADS + hh + 1, :]
        e1 = jnp.exp(s1 - m1)
        gt = gt + jnp.where(s1 + st_ref[2 * hh + 1] >= theta, e2_ref[hh], 0.0) * e1
    act = 0.5 * at * (1.0 + lax.erf(at * (2.0 ** -0.5)))
    h_tok = (gt * act).T.astype(BF16)
    o_ref[...] += _dot(h_tok, v_ref[...].astype(BF16))

    @pl.when(j == nj - 1)
    def _():
        o_ref[...] = _layer_norm_rows(alpha * h_ref[...] + o_ref[...], g2_ref[...], b2_ref[...])


def _peer(h1, st, rt, peer_u, peer_v, ln2_g, ln2_b, alpha):
    n, d = h1.shape
    nsk = st.shape[0]
    tb = _pick_tile(n, (1024, 512, 256, 128))
    nj = peer_u.shape[0] // N_KEYS
    return pl.pallas_call(
        functools.partial(_peer_kernel, alpha=alpha),
        name="peer_dense",
        grid=(n // tb, nj),
        in_specs=[pl.BlockSpec((tb, d), lambda i, j: (i, 0)),
                  pl.BlockSpec((nsk, N_KEYS, tb), lambda i, j: (0, 0, i)),
                  pl.BlockSpec((4 * PEER_HEADS, tb), lambda i, j: (0, i)),
                  pl.BlockSpec((N_KEYS, d), lambda i, j: (j, 0)),
                  pl.BlockSpec((N_KEYS, d), lambda i, j: (j, 0)),
                  pl.BlockSpec((1, d), lambda i, j: (0, 0)),
                  pl.BlockSpec((1, d), lambda i, j: (0, 0))],
        out_specs=pl.BlockSpec((tb, d), lambda i, j: (i, 0)),
        out_shape=jax.ShapeDtypeStruct((n, d), F32),
        scratch_shapes=[pltpu.VMEM((tb, d), BF16),
                        pltpu.VMEM((PEER_HEADS, N_KEYS, tb), F32)],
        compiler_params=_cparams(("parallel", "arbitrary")),
    )(h1, st, rt, peer_u, peer_v, ln2_g.reshape(1, d), ln2_b.reshape(1, d))


def kernel(x, meta_tokens, ln0_g, ln0_b, w_in, b_gate, pool_w, pool_scale, w_pool_up, w_attn_o, w_out, ln1_g, ln1_b, peer_wq, peer_sub_keys, peer_u, peer_v, ln2_g, ln2_b):
    b, s, d = x.shape
    depth = w_in.shape[0]
    t = N_META + s
    t_pad = -(-t // Q_BLOCK) * Q_BLOCK
    n = b * t_pad
    alpha = (2 * depth) ** 0.25
    topk = min(TOPK_CAP, s // 4)
    assert peer_u.shape[1] == N_KEYS * N_KEYS

    meta = jnp.broadcast_to(meta_tokens.astype(x.dtype)[None], (b, N_META, d))
    h = jnp.concatenate([meta, x, jnp.zeros((b, t_pad - t, d), x.dtype)], axis=1)
    h = _layer_norm(h.reshape(n, d), ln0_g, ln0_b)

    cq, sq = _rope_tables(t_pad, HEAD_DIM, HEAD_DIM ** -0.5)
    ck, sk = _rope_tables(t_pad, HEAD_DIM, 1.0)
    ci, si = _rope_tables(t_pad, IDX_ROPE, 1.0)

    for l in range(depth):
        w1, gate_off = _build_inproj_weight(w_in[l])
        xp, q, k, v, qi, ki, wi = _inproj(h.reshape(b, t_pad, d), w1, (cq, sq, ck, sk, ci, si))
        pool = _pool(xp, pool_w[l], pool_scale[l], w_pool_up[l])
        attn = _attention(qi, wi, ki, q, k, v, topk)
        h1, st = _merge(h, attn.reshape(n, -1), pool.reshape(n, d), w_in[l][:, gate_off:], b_gate[l],
                        w_attn_o[l], w_out[l], ln1_g[l], ln1_b[l], peer_wq[l], peer_sub_keys[l], alpha)
        rt = _route(st)
        h = _peer(h1, st, rt, peer_u[l], peer_v[l], ln2_g[l], ln2_b[l], alpha)
    return h.reshape(b, t_pad, d)[:, N_META:N_META + s]
```

```python
import functools
import math

import jax
import jax.numpy as jnp
from jax import lax
from jax.experimental import pallas as pl
from jax.experimental.pallas import tpu as pltpu

F32 = jnp.float32
BF16 = jnp.bfloat16

N_META = 16
POOL_WINDOWS = (2, 4, 8, 16)
POOL_GC = 128
POOL_WIDTH = POOL_GC * len(POOL_WINDOWS)
N_HEADS = 8
HEAD_DIM = 64
N_KV = 2
HEADS_PER_KV = N_HEADS // N_KV
IDX_HEADS = 4
IDX_DIM = 64
IDX_ROPE = 32
TOPK_CAP = 256
Q_BLOCK = 128
ROPE_THETA = 10000.0
N_KEYS = 128
PEER_HEADS = 8
PEER_HALF = 128
PEER_TOPK = 16
LN_EPS = 1e-5

LANES = 128
SUBLANES = 8
VMEM_LIMIT_BYTES = 56 * 1024 * 1024

NEG_INF = float("-inf")
POS_INF = float("inf")
BISECT_STEPS = 16

PEER_PAIRS = tuple((a, b) for a in range(PEER_TOPK) for b in range(PEER_TOPK)
                   if (a + 1) * (b + 1) <= PEER_TOPK)


def _pick_tile(n, candidates):
    for c in candidates:
        if n % c == 0:
            return c
    raise ValueError(f"no tile in {candidates} divides {n}")


def _cparams(sem):
    return pltpu.CompilerParams(dimension_semantics=sem, vmem_limit_bytes=VMEM_LIMIT_BYTES)


def _dot(a, b):
    return jnp.dot(a, b, preferred_element_type=F32)


def _dot_nt(a, b):
    return lax.dot_general(a, b, (((1,), (1,)), ((), ())), preferred_element_type=F32)


def _layer_norm_rows(x, g, b):
    mu = jnp.mean(x, axis=-1, keepdims=True)
    xc = x - mu
    var = jnp.mean(xc * xc, axis=-1, keepdims=True)
    return xc * lax.rsqrt(var + LN_EPS) * g + b


def _ln_kernel(x_ref, g_ref, b_ref, o_ref):
    o_ref[...] = _layer_norm_rows(x_ref[...], g_ref[...], b_ref[...])


def _layer_norm(x2, g, b):
    n, d = x2.shape
    tm = _pick_tile(n, (512, 256, 128))
    return pl.pallas_call(
        _ln_kernel,
        name="ln0",
        grid=(n // tm,),
        in_specs=[pl.BlockSpec((tm, d), lambda i: (i, 0)),
                  pl.BlockSpec((1, d), lambda i: (0, 0)),
                  pl.BlockSpec((1, d), lambda i: (0, 0))],
        out_specs=pl.BlockSpec((tm, d), lambda i: (i, 0)),
        out_shape=jax.ShapeDtypeStruct((n, d), F32),
        compiler_params=_cparams(("parallel",)),
    )(x2, g.reshape(1, d), b.reshape(1, d))


C_XP = 0
C_Q = C_XP + POOL_WIDTH
C_QS = C_Q + N_HEADS * LANES
C_K = C_QS + N_HEADS * LANES
C_KS = C_K + N_KV * LANES
C_V = C_KS + N_KV * LANES
C_QI = C_V + N_KV * HEAD_DIM
C_QIS = C_QI + IDX_HEADS * LANES
C_KI = C_QIS + IDX_HEADS * LANES
C_KIS = C_KI + LANES
C_WI = C_KIS + LANES
C_END = C_WI + LANES
WI_SCALE = IDX_HEADS ** -0.5 * IDX_DIM ** -0.5


def _inproj_kernel(h_ref, w_ref, cq_ref, sq_ref, ck_ref, sk_ref, ci_ref, si_ref,
                   xp_ref, q_ref, k_ref, v_ref, qi_ref, ki_ref, wi_ref):
    hb = h_ref[...].astype(BF16)

    def seg(c0, width):
        return _dot(hb, w_ref[:, c0:c0 + width])

    xp_ref[...] = seg(C_XP, POOL_WIDTH)
    v_ref[...] = seg(C_V, N_KV * HEAD_DIM).astype(BF16)
    wi_ref[...] = seg(C_WI, LANES) * WI_SCALE
    cq, sq = cq_ref[...], sq_ref[...]
    ck, sk = ck_ref[...], sk_ref[...]
    ci, si = ci_ref[...], si_ref[...]
    for hh in range(N_HEADS):
        a = seg(C_Q + hh * LANES, LANES)
        s = seg(C_QS + hh * LANES, LANES)
        q_ref[:, hh * LANES:(hh + 1) * LANES] = (a * cq + s * sq).astype(BF16)
    for g in range(N_KV):
        a = seg(C_K + g * LANES, LANES)
        s = seg(C_KS + g * LANES, LANES)
        k_ref[:, g * LANES:(g + 1) * LANES] = (a * ck + s * sk).astype(BF16)
    for hh in range(IDX_HEADS):
        a = seg(C_QI + hh * LANES, LANES)
        s = seg(C_QIS + hh * LANES, LANES)
        qi_ref[:, hh * LANES:(hh + 1) * LANES] = (a * ci + s * si).astype(BF16)
    ki_ref[...] = (seg(C_KI, LANES) * ci + seg(C_KIS, LANES) * si).astype(BF16)


def _pad_heads(w, n_heads, hd):
    d = w.shape[0]
    w3 = w.reshape(d, n_heads, hd)
    return jnp.pad(w3, ((0, 0), (0, 0), (0, LANES - hd))).reshape(d, n_heads * LANES)


def _swap_halves(w, n_heads, hd, rot):
    d = w.shape[0]
    half = rot // 2
    w3 = w.reshape(d, n_heads, hd)
    sw = jnp.concatenate([w3[..., half:rot], w3[..., :half], jnp.zeros_like(w3[..., rot:])], axis=-1)
    return sw.reshape(d, n_heads * hd)


def _rope_tables(t_pad, rot, scale):
    half = rot // 2
    inv = ROPE_THETA ** (-jnp.arange(half, dtype=F32) / half)
    ang = jnp.arange(t_pad, dtype=F32)[:, None] * inv[None, :]
    cos, sin = jnp.cos(ang), jnp.sin(ang)
    ones = jnp.ones((t_pad, LANES - rot), F32)
    c = jnp.concatenate([cos, cos, ones], axis=1) * scale
    s = jnp.concatenate([-sin, sin, jnp.zeros((t_pad, LANES - rot), F32)], axis=1) * scale
    return c, s


def _build_inproj_weight(w_in):
    o = 0
    parts = {}
    for name, width in (("xp", POOL_WIDTH), ("q", N_HEADS * HEAD_DIM), ("k", N_KV * HEAD_DIM),
                        ("v", N_KV * HEAD_DIM), ("qi", IDX_HEADS * IDX_DIM), ("ki", IDX_DIM),
                        ("wi", IDX_HEADS)):
        parts[name] = w_in[:, o:o + width]
        o += width
    d = w_in.shape[0]
    cols = [
        parts["xp"],
        _pad_heads(parts["q"], N_HEADS, HEAD_DIM),
        _pad_heads(_swap_halves(parts["q"], N_HEADS, HEAD_DIM, HEAD_DIM), N_HEADS, HEAD_DIM),
        _pad_heads(parts["k"], N_KV, HEAD_DIM),
        _pad_heads(_swap_halves(parts["k"], N_KV, HEAD_DIM, HEAD_DIM), N_KV, HEAD_DIM),
        parts["v"],
        _pad_heads(parts["qi"], IDX_HEADS, IDX_DIM),
        _pad_heads(_swap_halves(parts["qi"], IDX_HEADS, IDX_DIM, IDX_ROPE), IDX_HEADS, IDX_DIM),
        _pad_heads(parts["ki"], 1, IDX_DIM),
        _pad_heads(_swap_halves(parts["ki"], 1, IDX_DIM, IDX_ROPE), 1, IDX_DIM),
        jnp.pad(parts["wi"], ((0, 0), (0, LANES - IDX_HEADS))),
    ]
    w = jnp.concatenate(cols, axis=1).astype(BF16)
    assert w.shape == (d, C_END)
    return w, o


def _inproj(hn, w1, tables):
    b, t_pad, d = hn.shape
    tm = _pick_tile(t_pad, (272, 136, 128))
    row = lambda width: pl.BlockSpec((None, tm, width), lambda bi, i: (bi, i, 0))
    tab = pl.BlockSpec((tm, LANES), lambda bi, i: (i, 0))
    outs = [(POOL_WIDTH, F32), (N_HEADS * LANES, BF16), (N_KV * LANES, BF16), (N_KV * HEAD_DIM, BF16),
            (IDX_HEADS * LANES, BF16), (LANES, BF16), (LANES, F32)]
    return pl.pallas_call(
        _inproj_kernel,
        name="inproj",
        grid=(b, t_pad // tm),
        in_specs=[row(d), pl.BlockSpec((d, C_END), lambda bi, i: (0, 0))] + [tab] * 6,
        out_specs=[row(wd) for wd, _ in outs],
        out_shape=[jax.ShapeDtypeStruct((b, t_pad, wd), dt) for wd, dt in outs],
        compiler_params=_cparams(("parallel", "parallel")),
    )(hn, w1, *tables)


HALO = 16


def _pool_kernel(xp_ref, pw_ref, ps_ref, wup_ref, o_ref, ext_ref, *, tm):
    i = pl.program_id(1)

    @pl.when(i == 0)
    def _():
        ext_ref[0:HALO, :] = jnp.zeros((HALO, POOL_WIDTH), F32)

    @pl.when(i > 0)
    def _():
        ext_ref[0:HALO, :] = ext_ref[tm:tm + HALO, :]

    ext_ref[HALO:HALO + tm, :] = xp_ref[...]
    t1 = (i * tm + 1 + lax.broadcasted_iota(jnp.int32, (tm, 1), 0)).astype(F32)
    ys = []
    for g, w in enumerate(POOL_WINDOWS):
        lanes = slice(g * POOL_GC, (g + 1) * POOL_GC)
        x = ext_ref[HALO:HALO + tm, lanes]
        acc = x
        for j in range(1, w):
            acc = acc + ext_ref[HALO - j:HALO - j + tm, lanes]
        cnt = jnp.minimum(t1, float(w))
        mixed = acc / cnt - x
        ys.append(_dot(mixed.astype(BF16), pw_ref[g]))
    y = jnp.concatenate(ys, axis=1) * ps_ref[...]
    o_ref[...] = _dot(y.astype(BF16), wup_ref[...])


def _pool(xp, pool_w, pool_scale, w_pool_up):
    b, t_pad, _ = xp.shape
    d = w_pool_up.shape[1]
    tm = _pick_tile(t_pad, (272, 136, 128))
    return pl.pallas_call(
        functools.partial(_pool_kernel, tm=tm),
        name="pool",
        grid=(b, t_pad // tm),
        in_specs=[pl.BlockSpec((None, tm, POOL_WIDTH), lambda bi, i: (bi, i, 0)),
                  pl.BlockSpec((len(POOL_WINDOWS), POOL_GC, POOL_GC), lambda bi, i: (0, 0, 0)),
                  pl.BlockSpec((1, POOL_WIDTH), lambda bi, i: (0, 0)),
                  pl.BlockSpec((POOL_WIDTH, d), lambda bi, i: (0, 0))],
        out_specs=pl.BlockSpec((None, tm, d), lambda bi, i: (bi, i, 0)),
        out_shape=jax.ShapeDtypeStruct((b, t_pad, d), F32),
        scratch_shapes=[pltpu.VMEM((tm + HALO, POOL_WIDTH), F32)],
        compiler_params=_cparams(("parallel", "arbitrary")),
    )(xp, pool_w.astype(BF16), pool_scale.reshape(1, POOL_WIDTH), w_pool_up.astype(BF16))


CHUNK_UNROLL = 4


def _chunk_loop(n, body, init):
    def group(i, c):
        for u in range(CHUNK_UNROLL):
            c = body(i * CHUNK_UNROLL + u, c)
        return c

    main = n // CHUNK_UNROLL
    c = lax.fori_loop(0, main, group, init)
    return lax.fori_loop(main * CHUNK_UNROLL, n, body, c)


def _attn_kernel(qi_ref, wi_ref, ki_ref, q_ref, k_ref, v_ref, o_ref,
                 vt_ref, isum_ref, bias_ref, sc_ref, ot_ref, *, nb, topk):
    qb = pl.program_id(1)
    nkb = qb + 1
    kf = float(topk)

    @pl.when(qb == 0)
    def _():
        for c in range(nb):
            blk = v_ref[c * Q_BLOCK:(c + 1) * Q_BLOCK, :].astype(F32)
            vt_ref[c] = blk.T.astype(BF16)

    rows = lax.broadcasted_iota(jnp.int32, (Q_BLOCK, Q_BLOCK), 0)
    cols = lax.broadcasted_iota(jnp.int32, (Q_BLOCK, Q_BLOCK), 1)
    tri_causal = rows <= cols
    tri_prefix = (cols <= rows).astype(BF16)

    wt = wi_ref[...].T
    w_all = jnp.concatenate([wt[hh:hh + 1, :] for hh in range(IDX_HEADS)], axis=1)
    qi_all = jnp.concatenate([qi_ref[:, hh * LANES:(hh + 1) * LANES] for hh in range(IDX_HEADS)],
                             axis=0)

    def idx_body(kb, carry):
        xmax, xmin = carry
        start = pl.multiple_of(kb * Q_BLOCK, Q_BLOCK)
        kic = ki_ref[pl.ds(start, Q_BLOCK), :]
        sc = jnp.maximum(_dot_nt(kic, qi_all), 0.0) * w_all
        acc = sc[:, 0:Q_BLOCK]
        for hh in range(1, IDX_HEADS):
            acc = acc + sc[:, hh * Q_BLOCK:(hh + 1) * Q_BLOCK]
        causal = jnp.logical_or(kb < qb, tri_causal)
        isum_ref[kb] = jnp.where(causal, acc, NEG_INF)
        xmax = jnp.maximum(xmax, jnp.max(jnp.where(causal, acc, NEG_INF), axis=0, keepdims=True))
        xmin = jnp.minimum(xmin, jnp.min(jnp.where(causal, acc, POS_INF), axis=0, keepdims=True))
        return xmax, xmin

    xmax, xmin = _chunk_loop(
        nkb, idx_body,
        (jnp.full((1, Q_BLOCK), NEG_INF, F32), jnp.full((1, Q_BLOCK), POS_INF, F32)))

    def count_ge(t):
        def body(kb, c):
            return c + jnp.sum(jnp.where(isum_ref[kb] >= t, 1.0, 0.0), axis=0, keepdims=True)
        return _chunk_loop(nkb, body, jnp.zeros((1, Q_BLOCK), F32))

    def count_and_next(t):
        def body(kb, c):
            cnt, nxt = c
            x = isum_ref[kb]
            cnt = cnt + jnp.sum(jnp.where(x >= t, 1.0, 0.0), axis=0, keepdims=True)
            nxt = jnp.maximum(nxt, jnp.max(jnp.where(x < t, x, NEG_INF), axis=0, keepdims=True))
            return cnt, nxt
        return _chunk_loop(nkb, body, (jnp.zeros((1, Q_BLOCK), F32),
                                       jnp.full((1, Q_BLOCK), NEG_INF, F32)))

    def search(_):
        c_hi = count_ge(xmax)

        def bis(_, c):
            lo, hi = c
            mid = lo + 0.5 * (hi - lo)
            ge = count_ge(mid) >= kf
            return jnp.where(ge, mid, lo), jnp.where(ge, hi, mid)

        lo, hi = lax.fori_loop(0, BISECT_STEPS, bis, (xmin, xmax))
        _, below_hi = count_and_next(hi)
        t0 = jnp.where(c_hi >= kf, xmax, below_hi)

        def wcond(c):
            return c[1] > 0.0

        def wbody(c):
            t, _ = c
            cnt, nxt = count_and_next(t)
            need = cnt < kf
            return jnp.where(need, nxt, t), jnp.max(jnp.where(need, 1.0, 0.0))

        thr, _ = lax.while_loop(wcond, wbody, (t0, jnp.float32(1.0)))
        return thr

    thr = lax.cond(nkb * Q_BLOCK > topk, search,
                   lambda _: jnp.full((1, Q_BLOCK), NEG_INF, F32), 0)

    def count_gt(kb, c):
        return c + jnp.sum(jnp.where(isum_ref[kb] > thr, 1.0, 0.0), axis=0, keepdims=True)

    room = kf - _chunk_loop(nkb, count_gt, jnp.zeros((1, Q_BLOCK), F32))
    tie_ok = thr > NEG_INF

    def bias_body(kb, carry):
        x = isum_ref[kb]
        eq = x == thr
        pref = _dot(tri_prefix, jnp.where(eq, 1.0, 0.0).astype(BF16)) + carry
        sel = jnp.logical_or(x > thr, jnp.logical_and(jnp.logical_and(eq, tie_ok), pref <= room))
        bias_ref[kb] = jnp.where(sel, 0.0, NEG_INF)
        return pref[Q_BLOCK - 1:Q_BLOCK, :]

    _chunk_loop(nkb, bias_body, jnp.zeros((1, Q_BLOCK), F32))

    gw = HEADS_PER_KV * Q_BLOCK
    for g in range(N_KV):
        qg = jnp.concatenate([q_ref[:, (g * HEADS_PER_KV + r) * LANES:(g * HEADS_PER_KV + r + 1) * LANES]
                              for r in range(HEADS_PER_KV)], axis=0)

        def s_body(kb, m):
            start = pl.multiple_of(kb * Q_BLOCK, Q_BLOCK)
            kc = k_ref[pl.ds(start, Q_BLOCK), g * LANES:(g + 1) * LANES]
            s = _dot_nt(kc, qg) + jnp.concatenate([bias_ref[kb]] * HEADS_PER_KV, axis=1)
            sc_ref[kb] = s
            return jnp.maximum(m, jnp.max(s, axis=0, keepdims=True))

        m = _chunk_loop(nkb, s_body, jnp.full((1, gw), NEG_INF, F32))

        def p_body(kb, c):
            l, acc = c
            p = jnp.exp(sc_ref[kb] - m)
            l = l + jnp.sum(p, axis=0, keepdims=True)
            vt = vt_ref[kb][g * HEAD_DIM:(g + 1) * HEAD_DIM, :]
            return l, acc + _dot(vt, p.astype(BF16))

        l, acc = _chunk_loop(nkb, p_body, (jnp.zeros((1, gw), F32),
                                           jnp.zeros((HEAD_DIM, gw), F32)))
        out = acc / l
        for r in range(HEADS_PER_KV):
            hh = g * HEADS_PER_KV + r
            ot_ref[hh * HEAD_DIM:(hh + 1) * HEAD_DIM, :] = out[:, r * Q_BLOCK:(r + 1) * Q_BLOCK]

    o_ref[...] = ot_ref[...].T.astype(BF16)


def _attention(qi, wi, ki, q, k, v, topk):
    b, t_pad, _ = q.shape
    nb = t_pad // Q_BLOCK
    blk = lambda width: pl.BlockSpec((None, Q_BLOCK, width), lambda bi, i: (bi, i, 0))
    full = lambda width: pl.BlockSpec((None, t_pad, width), lambda bi, i: (bi, 0, 0))
    qw = N_HEADS * HEAD_DIM
    return pl.pallas_call(
        functools.partial(_attn_kernel, nb=nb, topk=topk),
        name="dsa_attn",
        grid=(b, nb),
        in_specs=[blk(IDX_HEADS * LANES), blk(LANES), full(LANES),
                  blk(N_HEADS * LANES), full(N_KV * LANES), full(N_KV * HEAD_DIM)],
        out_specs=blk(qw),
        out_shape=jax.ShapeDtypeStruct((b, t_pad, qw), BF16),
        scratch_shapes=[pltpu.VMEM((nb, N_KV * HEAD_DIM, Q_BLOCK), BF16),
                        pltpu.VMEM((nb, Q_BLOCK, Q_BLOCK), F32),
                        pltpu.VMEM((nb, Q_BLOCK, Q_BLOCK), F32),
                        pltpu.VMEM((nb, Q_BLOCK, HEADS_PER_KV * Q_BLOCK), F32),
                        pltpu.VMEM((qw, Q_BLOCK), F32)],
        compiler_params=_cparams(("parallel", "arbitrary")),
    )(qi, wi, ki, q, k, v)


def _merge_kernel(hn_ref, attn_ref, pool_ref, wg_ref, bg_ref, wo_ref, wout_ref, g1_ref, b1_ref,
                  wq_ref, sk_ref, h1_ref, s1_ref, s2_ref, *, alpha, d):
    hn = hn_ref[...]
    gates = _dot(hn.astype(BF16), wg_ref[...]) + bg_ref[...]
    attn = _dot(attn_ref[...], wo_ref[...])
    merged = jax.nn.sigmoid(gates[:, :d]) * pool_ref[...] + jax.nn.sigmoid(gates[:, d:]) * attn
    mix = _dot(merged.astype(BF16), wout_ref[...])
    h1 = _layer_norm_rows(alpha * hn + mix, g1_ref[...], b1_ref[...])
    h1_ref[...] = h1
    qp = _dot(h1.astype(BF16), wq_ref[...]).astype(BF16)
    for c in range(2 * PEER_HEADS):
        dst = s1_ref if c % 2 == 0 else s2_ref
        dst[c // 2] = _dot_nt(sk_ref[c], qp[:, c * PEER_HALF:(c + 1) * PEER_HALF])


def _merge(hn2, attn2, pool2, w_g, b_gate, w_attn_o, w_out, ln1_g, ln1_b, peer_wq, sub_keys, alpha):
    n, d = hn2.shape
    tm = _pick_tile(n, (256, 128))
    row = lambda width: pl.BlockSpec((tm, width), lambda i: (i, 0))
    const = lambda shape: pl.BlockSpec(shape, lambda i: (0,) * len(shape))
    nsk = 2 * PEER_HEADS
    qw = attn2.shape[1]
    return pl.pallas_call(
        functools.partial(_merge_kernel, alpha=alpha, d=d),
        name="merge",
        grid=(n // tm,),
        in_specs=[row(d), row(qw), row(d),
                  const((d, 2 * d)), const((1, 2 * d)), const((qw, d)), const((d, d)),
                  const((1, d)), const((1, d)), const((d, nsk * PEER_HALF)),
                  const((nsk, N_KEYS, PEER_HALF))],
        out_specs=[row(d)] + [pl.BlockSpec((PEER_HEADS, N_KEYS, tm), lambda i: (0, 0, i))] * 2,
        out_shape=[jax.ShapeDtypeStruct((n, d), F32)]
        + [jax.ShapeDtypeStruct((PEER_HEADS, N_KEYS, n), F32)] * 2,
        compiler_params=_cparams(("parallel",)),
    )(hn2, attn2, pool2, w_g.astype(BF16), b_gate.reshape(1, 2 * d), w_attn_o.astype(BF16),
      w_out.astype(BF16), ln1_g.reshape(1, d), ln1_b.reshape(1, d), peer_wq.astype(BF16),
      sub_keys.reshape(nsk, N_KEYS, PEER_HALF).astype(BF16))


def _top_values(x, k):
    out = []
    for _ in range(k):
        m = jnp.max(x, axis=0, keepdims=True)
        out.append(m)
        x = jnp.where(x == m, NEG_INF, x)
    return out


def _route_kernel(s1_ref, s2_ref, cut_ref, e1_ref, e2_ref):
    kf = float(PEER_TOPK)
    for hh in range(PEER_HEADS):
        s1 = s1_ref[hh]
        s2 = s2_ref[hh]
        v1 = _top_values(s1, PEER_TOPK)
        v2 = _top_values(s2, PEER_TOPK)
        pad_rows = -len(PEER_PAIRS) % SUBLANES
        cand = jnp.concatenate([v1[a] + v2[b] for a, b in PEER_PAIRS]
                               + [jnp.full_like(v1[0], NEG_INF)] * pad_rows, axis=0)
        top = v1[0] + v2[0]
        x = cand
        theta = jnp.full_like(top, NEG_INF)
        left = jnp.full_like(top, kf)
        for _ in range(PEER_TOPK):
            m = jnp.max(x, axis=0, keepdims=True)
            hit = x == m
            theta = jnp.where(left > 0.0, m, theta)
            left = left - jnp.sum(jnp.where(hit, 1.0, 0.0), axis=0, keepdims=True)
            x = jnp.where(hit, NEG_INF, x)
        z = jnp.sum(jnp.where(cand >= theta, jnp.exp(cand - top), 0.0), axis=0, keepdims=True)
        cut = jnp.full_like(s1, POS_INF)
        for b in range(PEER_TOPK):
            cut = jnp.where(s1 + v2[b] >= theta, v2[b], cut)
        cut_ref[hh] = cut
        e1_ref[hh] = jnp.exp(s1 - v1[0])
        e2_ref[hh] = jnp.exp(s2 - v2[0]) * (1.0 / z)


def _route(s1, s2):
    nh, nk, n = s1.shape
    tb = _pick_tile(n, (256, 128))
    spec = pl.BlockSpec((nh, nk, tb), lambda i: (0, 0, i))
    return pl.pallas_call(
        _route_kernel,
        name="peer_route",
        grid=(n // tb,),
        in_specs=[spec, spec],
        out_specs=[spec] * 3,
        out_shape=[jax.ShapeDtypeStruct((nh, nk, n), F32)] * 3,
        compiler_params=_cparams(("parallel",)),
    )(s1, s2)


PEER_EB = 256
PEER_TC = 256
PEER_TK = 2048


def _peer_hidden_kernel(h_ref, s2_ref, cut_ref, e1_ref, e2_ref, u_ref, o_ref, xb_ref, a_ref, *, tb):
    j = pl.program_id(1)
    nsub = PEER_EB // N_KEYS

    @pl.when(j == 0)
    def _():
        xb_ref[...] = h_ref[...].astype(BF16)

    a_ref[...] = _dot_nt(u_ref[...], xb_ref[...])
    for s in range(nsub):
        row = j * nsub + s
        exp_rows = slice(s * N_KEYS, (s + 1) * N_KEYS)
        for c in range(tb // PEER_TC):
            tok = slice(c * PEER_TC, (c + 1) * PEER_TC)
            gt = None
            for hh in range(PEER_HEADS):
                keep = s2_ref[hh, :, tok] >= cut_ref[hh, pl.ds(row, 1), tok]
                term = jnp.where(keep, e2_ref[hh, :, tok], 0.0) * e1_ref[hh, pl.ds(row, 1), tok]
                gt = term if gt is None else gt + term
            at = a_ref[exp_rows, tok]
            act = 0.5 * at * (1.0 + lax.erf(at * (2.0 ** -0.5)))
            o_ref[tok, exp_rows] = (gt * act).T.astype(BF16)


def _peer_out_kernel(h_ref, hid_ref, v_ref, g2_ref, b2_ref, o_ref, *, alpha):
    kk = pl.program_id(1)

    @pl.when(kk == 0)
    def _():
        o_ref[...] = jnp.zeros_like(o_ref)

    o_ref[...] += _dot(hid_ref[...], v_ref[...])

    @pl.when(kk == pl.num_programs(1) - 1)
    def _():
        o_ref[...] = _layer_norm_rows(alpha * h_ref[...] + o_ref[...], g2_ref[...], b2_ref[...])


def _peer(h1, s2, cut, e1, e2, peer_u, peer_v, ln2_g, ln2_b, alpha):
    n, d = h1.shape
    ne = peer_u.shape[0]
    tb = _pick_tile(n, (1024, 512, 256))
    once = pl.Buffered(1)
    tile3 = pl.BlockSpec((PEER_HEADS, N_KEYS, tb), lambda i, j: (0, 0, i), pipeline_mode=once)
    hidden = pl.pallas_call(
        functools.partial(_peer_hidden_kernel, tb=tb),
        name="peer_hidden",
        grid=(n // tb, ne // PEER_EB),
        in_specs=[pl.BlockSpec((tb, d), lambda i, j: (i, 0), pipeline_mode=once),
                  tile3, tile3, tile3, tile3,
                  pl.BlockSpec((PEER_EB, d), lambda i, j: (j, 0))],
        out_specs=pl.BlockSpec((tb, PEER_EB), lambda i, j: (i, j)),
        out_shape=jax.ShapeDtypeStruct((n, ne), BF16),
        scratch_shapes=[pltpu.VMEM((tb, d), BF16),
                        pltpu.VMEM((PEER_EB, tb), F32)],
        compiler_params=_cparams(("parallel", "arbitrary")),
    )(h1, s2, cut, e1, e2, peer_u.astype(BF16))
    return pl.pallas_call(
        functools.partial(_peer_out_kernel, alpha=alpha),
        name="peer_out",
        grid=(n // tb, ne // PEER_TK),
        in_specs=[pl.BlockSpec((tb, d), lambda i, kk: (i, 0), pipeline_mode=once),
                  pl.BlockSpec((tb, PEER_TK), lambda i, kk: (i, kk)),
                  pl.BlockSpec((PEER_TK, d), lambda i, kk: (kk, 0)),
                  pl.BlockSpec((1, d), lambda i, kk: (0, 0)),
                  pl.BlockSpec((1, d), lambda i, kk: (0, 0))],
        out_specs=pl.BlockSpec((tb, d), lambda i, kk: (i, 0)),
        out_shape=jax.ShapeDtypeStruct((n, d), F32),
        compiler_params=_cparams(("parallel", "arbitrary")),
    )(h1, hidden, peer_v.astype(BF16), ln2_g.reshape(1, d), ln2_b.reshape(1, d))


def kernel(x, meta_tokens, ln0_g, ln0_b, w_in, b_gate, pool_w, pool_scale, w_pool_up, w_attn_o, w_out, ln1_g, ln1_b, peer_wq, peer_sub_keys, peer_u, peer_v, ln2_g, ln2_b):
    b, s, d = x.shape
    depth = w_in.shape[0]
    t = N_META + s
    t_pad = -(-t // Q_BLOCK) * Q_BLOCK
    n = b * t_pad
    alpha = (2 * depth) ** 0.25
    topk = min(TOPK_CAP, s // 4)
    assert peer_u.shape[1] == N_KEYS * N_KEYS

    meta = jnp.broadcast_to(meta_tokens.astype(x.dtype)[None], (b, N_META, d))
    h = jnp.concatenate([meta, x, jnp.zeros((b, t_pad - t, d), x.dtype)], axis=1)
    h = _layer_norm(h.reshape(n, d), ln0_g, ln0_b)

    cq, sq = _rope_tables(t_pad, HEAD_DIM, HEAD_DIM ** -0.5)
    ck, sk = _rope_tables(t_pad, HEAD_DIM, 1.0)
    ci, si = _rope_tables(t_pad, IDX_ROPE, 1.0)

    for l in range(depth):
        w1, gate_off = _build_inproj_weight(w_in[l])
        xp, q, k, v, qi, ki, wi = _inproj(h.reshape(b, t_pad, d), w1, (cq, sq, ck, sk, ci, si))
        pool = _pool(xp, pool_w[l], pool_scale[l], w_pool_up[l])
        attn = _attention(qi, wi, ki, q, k, v, topk)
        h1, s1, s2 = _merge(h, attn.reshape(n, -1), pool.reshape(n, d), w_in[l][:, gate_off:],
                            b_gate[l], w_attn_o[l], w_out[l], ln1_g[l], ln1_b[l], peer_wq[l],
                            peer_sub_keys[l], alpha)
        cut, e1, e2 = _route(s1, s2)
        h = _peer(h1, s2, cut, e1, e2, peer_u[l], peer_v[l], ln2_g[l], ln2_b[l], alpha)
    return h.reshape(b, t_pad, d)[:, N_META:N_META + s]
```
